```python
import math
import jax
import jax.numpy as jnp
from jax import lax
import numpy as np

D_MODEL = 1024
BATCH = 32
SEQ = 2048
DEPTH = 1
DEC_BATCH = 32
DEC_SEQ = 64
PAST_LEN = 1024

CHUNK = 64
Q_BLOCK = 128
ROPE_THETA = 10000.0
EPS = 1e-5
F32 = jnp.float32
N_HEADS_A = 8
HEAD_DIM_A = D_MODEL // (2 * N_HEADS_A)
V_DIM_A = 2 * HEAD_DIM_A
ATTN_WIDTH = N_HEADS_A * V_DIM_A
SCALE_A = HEAD_DIM_A ** -0.5
EXPAND = 2
D_INNER = EXPAND * D_MODEL
HEAD_DIM_S = 64
N_HEADS_S = D_INNER // HEAD_DIM_S
N_GROUPS_S = 4
HEADS_PER_GROUP = N_HEADS_S // N_GROUPS_S
D_STATE = 128
CONV_W = 4
CONV_DIM = D_INNER + 2 * N_GROUPS_S * D_STATE
D_FF = 256 * (-(-(8 * D_MODEL) // (3 * 256)))
OFF_K = 2 * N_HEADS_A * HEAD_DIM_A
OFF_V = OFF_K + 2 * N_HEADS_A * HEAD_DIM_A
OFF_Z = OFF_V + ATTN_WIDTH
OFF_XBC = OFF_Z + D_INNER
OFF_DT = OFF_XBC + CONV_DIM
OFF_GATE = OFF_DT + N_HEADS_S
IN_COLS = OFF_GATE + 2 * D_MODEL

kernel_name = "diffattn_ssd_gated_hybrid_stream_step"


def rmsnorm(x, g):
    xf = x.astype(F32)
    xf = xf * lax.rsqrt(jnp.mean(xf * xf, axis=-1, keepdims=True) + EPS)
    return (xf * g.astype(F32)).astype(x.dtype)


def rope(x, pos):
    half = x.shape[-1] // 2
    inv = ROPE_THETA ** (-jnp.arange(half, dtype=F32) / half)
    ang = pos.astype(F32)[:, None] * inv[None, :]
    cos = jnp.cos(ang)[None, :, None, :]
    sin = jnp.sin(ang)[None, :, None, :]
    xf = x.astype(F32)
    x1, x2 = xf[..., :half], xf[..., half:]
    return jnp.concatenate([x1 * cos - x2 * sin, x2 * cos + x1 * sin], axis=-1).astype(x.dtype)


def diff_combine(probs, v, lam):
    b, _, tq, tk = probs.shape
    p = probs.reshape(b, N_HEADS_A, 2, tq, tk)
    w = p[:, :, 0] - lam * p[:, :, 1]
    return jnp.einsum("bhqk,bkhe->bqhe", w, v.astype(F32))


def diff_attn_prompt(q, k, v, lam):
    b, t = q.shape[:2]
    nb = t // Q_BLOCK
    qb = q.reshape(b, nb, Q_BLOCK, 2 * N_HEADS_A, HEAD_DIM_A).swapaxes(0, 1)
    key_chunk = jnp.arange(t) // CHUNK

    def one_block(args):
        qi, blk = args
        q_chunk = (blk * Q_BLOCK + jnp.arange(Q_BLOCK)) // CHUNK
        mask = key_chunk[None, :] <= q_chunk[:, None]
        s = jnp.einsum("bqhd,bkhd->bhqk", qi, k, preferred_element_type=F32) * SCALE_A
        s = jnp.where(mask[None, None], s, -jnp.inf)
        return diff_combine(jax.nn.softmax(s, axis=-1), v, lam)

    out = lax.map(one_block, (qb, jnp.arange(nb)))
    return out.swapaxes(0, 1).reshape(b, t, N_HEADS_A, V_DIM_A)


def diff_attn_full(q, k, v, lam):
    s = jnp.einsum("bqhd,bkhd->bhqk", q, k, preferred_element_type=F32) * SCALE_A
    return diff_combine(jax.nn.softmax(s, axis=-1), v, lam)


def causal_conv(u, buf, w, bias):
    t = u.shape[1]
    full = jnp.concatenate([buf.astype(u.dtype), u], axis=1)
    out = bias.astype(u.dtype) + full[:, 0:t] * w[0]
    for j in range(1, CONV_W):
        out = out + full[:, j:j + t] * w[j]
    return out, full[:, -(CONV_W - 1):].astype(buf.dtype)


def ssd_scan(xdt, dA, bm, cm, s0, chunk):
    b, t = xdt.shape[:2]
    nc = t // chunk

    def to_chunks(a):
        a = a.astype(F32)
        return a.reshape((b, nc, chunk) + a.shape[2:]).swapaxes(0, 1)

    xs = (to_chunks(xdt.reshape(b, t, N_GROUPS_S, HEADS_PER_GROUP, HEAD_DIM_S)),
          to_chunks(dA.reshape(b, t, N_GROUPS_S, HEADS_PER_GROUP)),
          to_chunks(bm), to_chunks(cm))
    causal = jnp.tril(jnp.ones((chunk, chunk), dtype=bool))[None, :, :, None, None]

    def step(state, inp):
        xc, ac, bc, cc = inp
        cum = jnp.cumsum(ac, axis=1)
        seg = cum[:, :, None] - cum[:, None, :]
        decay = jnp.exp(jnp.where(causal, seg, -jnp.inf))
        cb = jnp.einsum("blgn,bsgn->blsg", cc, bc)
        y_diag = jnp.einsum("blsg,blsgk,bsgkp->blgkp", cb, decay, xc)
        y_off = jnp.einsum("blgn,bgkpn->blgkp", cc, state) * jnp.exp(cum)[..., None]
        to_end = jnp.exp(cum[:, -1:] - cum)
        new_state = (state * jnp.exp(cum[:, -1])[..., None, None]
                     + jnp.einsum("bsgn,bsgk,bsgkp->bgkpn", bc, to_end, xc))
        return new_state, y_diag + y_off

    init = s0.astype(F32).reshape(b, N_GROUPS_S, HEADS_PER_GROUP, HEAD_DIM_S, D_STATE)
    s_fin, ys = lax.scan(step, init, xs)
    y = ys.swapaxes(0, 1).reshape(b, t, N_HEADS_S, HEAD_DIM_S)
    return y, s_fin.reshape(b, N_HEADS_S, HEAD_DIM_S, D_STATE)


def gated_group_rmsnorm(y, z, g):
    b, t, _ = y.shape
    u = (y.astype(F32) * jax.nn.silu(z.astype(F32))).reshape(b, t, N_GROUPS_S, D_INNER // N_GROUPS_S)
    u = u * lax.rsqrt(jnp.mean(u * u, axis=-1, keepdims=True) + EPS)
    return u.reshape(b, t, D_INNER) * g.astype(F32)


def trunk_layer(x, c, pos, k_past, v_past, conv_buf, ssm_state, lp, lam_init):
    (w_ada, b_ada, g_mix, g_ffn, w_in, lq1, lk1, lq2, lk2, g_subln, conv_w, conv_b,
     dt_bias, a_log, d_skip, g_ssd, w_pa, w_pb, w_out, w_gu, w_down) = lp
    b, t, _ = x.shape
    dt_x = x.dtype
    mod = jnp.einsum("bd,de->be", jax.nn.silu(c), w_ada) + b_ada
    sh_m, sc_m, gt_m, sh_f, sc_f, gt_f = jnp.split(mod[:, None, :], 6, axis=-1)

    h = rmsnorm(x, g_mix) * (1 + sc_m) + sh_m
    proj = jnp.einsum("btd,de->bte", h, w_in)
    q, k, v, z, xbc, dt_raw, gates = jnp.split(
        proj, [OFF_K, OFF_V, OFF_Z, OFF_XBC, OFF_DT, OFF_GATE], axis=-1)

    q = rope(q.reshape(b, t, 2 * N_HEADS_A, HEAD_DIM_A), pos)
    k = rope(k.reshape(b, t, 2 * N_HEADS_A, HEAD_DIM_A), pos)
    v = v.reshape(b, t, N_HEADS_A, V_DIM_A)
    lam = (jnp.exp(jnp.sum(lq1.astype(F32) * lk1.astype(F32)))
           - jnp.exp(jnp.sum(lq2.astype(F32) * lk2.astype(F32))) + lam_init)
    if k_past is None:
        o = diff_attn_prompt(q, k, v, lam)
    else:
        o = diff_attn_full(q, jnp.concatenate([k_past.astype(k.dtype), k], axis=1),
                           jnp.concatenate([v_past.astype(v.dtype), v], axis=1), lam)
    o = rmsnorm(o, g_subln) * (1.0 - lam_init)
    y_a = jnp.einsum("bte,ed->btd", o.reshape(b, t, ATTN_WIDTH).astype(dt_x), w_pa)

    xbc_c, conv_new = causal_conv(xbc, conv_buf, conv_w, conv_b)
    xbc_c = jax.nn.silu(xbc_c)
    xs, bm, cm = jnp.split(xbc_c, [D_INNER, D_INNER + N_GROUPS_S * D_STATE], axis=-1)
    xs = xs.reshape(b, t, N_HEADS_S, HEAD_DIM_S)
    bm = bm.reshape(b, t, N_GROUPS_S, D_STATE)
    cm = cm.reshape(b, t, N_GROUPS_S, D_STATE)
    dt = jax.nn.softplus(dt_raw.astype(F32) + dt_bias.astype(F32))
    a = -jnp.exp(a_log.astype(F32))
    y_s, ssm_new = ssd_scan(xs.astype(F32) * dt[..., None], dt * a, bm, cm, ssm_state, min(CHUNK, t))
    y_s = y_s + d_skip.astype(F32)[:, None] * xs.astype(F32)
    y_s = gated_group_rmsnorm(y_s.reshape(b, t, D_INNER), z, g_ssd).astype(dt_x)
    y_b = jnp.einsum("bte,ed->btd", y_s, w_pb)

    g_a, g_b = jnp.split(gates, 2, axis=-1)
    merged = jax.nn.sigmoid(g_a) * y_a + jax.nn.sigmoid(g_b) * y_b
    x = x + gt_m * jnp.einsum("btd,de->bte", merged, w_out)

    h2 = rmsnorm(x, g_ffn) * (1 + sc_f) + sh_f
    gate_ff, up_ff = jnp.split(jnp.einsum("btd,df->btf", h2, w_gu), 2, axis=-1)
    x = x + gt_f * jnp.einsum("btf,fd->btd", jax.nn.silu(gate_ff) * up_ff, w_down)
    return x.astype(dt_x), (k, v, conv_new, ssm_new.astype(ssm_state.dtype))


def setup_inputs(seed: int = 0) -> dict:
    key = jax.random.key(seed)
    ks = jax.random.split(key, 32)
    nrm = lambda k, shape, s: jax.random.normal(k, shape, F32) * s
    gain = lambda k, shape: 1.0 + 0.01 * jax.random.normal(k, shape, F32)
    u = jax.random.uniform(ks[20], (DEPTH, N_HEADS_S), F32)
    dt0 = jnp.exp(u * (math.log(0.1) - math.log(1e-3)) + math.log(1e-3))
    return {
        "x_prompt": nrm(ks[0], (BATCH, SEQ, D_MODEL), 1.0),
        "x_sample": nrm(ks[1], (DEC_BATCH, DEC_SEQ, D_MODEL), 1.0),
        "cache_k": nrm(ks[2], (DEPTH, DEC_BATCH, PAST_LEN, 2 * N_HEADS_A, HEAD_DIM_A), 1.0),
        "cache_v": nrm(ks[3], (DEPTH, DEC_BATCH, PAST_LEN, N_HEADS_A, V_DIM_A), 1.0),
        "state_conv": nrm(ks[4], (DEPTH, DEC_BATCH, CONV_W - 1, CONV_DIM), 1.0),
        "state_ssm": nrm(ks[5], (DEPTH, DEC_BATCH, N_HEADS_S, HEAD_DIM_S, D_STATE), 0.1),
        "c_prompt": nrm(ks[6], (BATCH, D_MODEL), 1.0),
        "c_sample": nrm(ks[7], (DEC_BATCH, D_MODEL), 1.0),
        "w_ada": nrm(ks[8], (DEPTH, D_MODEL, 6 * D_MODEL), 0.5 * D_MODEL ** -0.5),
        "b_ada": nrm(ks[9], (DEPTH, 6 * D_MODEL), 0.01),
        "g_mix": gain(ks[10], (DEPTH, D_MODEL)),
        "g_ffn": gain(ks[11], (DEPTH, D_MODEL)),
        "w_in": nrm(ks[12], (DEPTH, D_MODEL, IN_COLS), D_MODEL ** -0.5),
        "lambda_q1": nrm(ks[13], (DEPTH, HEAD_DIM_A), 0.1),
        "lambda_k1": nrm(ks[14], (DEPTH, HEAD_DIM_A), 0.1),
        "lambda_q2": nrm(ks[15], (DEPTH, HEAD_DIM_A), 0.1),
        "lambda_k2": nrm(ks[16], (DEPTH, HEAD_DIM_A), 0.1),
        "g_subln": gain(ks[17], (DEPTH, V_DIM_A)),
        "conv_w": nrm(ks[18], (DEPTH, CONV_W, CONV_DIM), CONV_W ** -0.5),
        "conv_b": nrm(ks[19], (DEPTH, CONV_DIM), 0.01),
        "dt_bias": dt0 + jnp.log(-jnp.expm1(-dt0)),
        "a_log": jnp.log(jax.random.uniform(ks[21], (DEPTH, N_HEADS_S), F32, 1.0, 16.0)),
        "d_skip": gain(ks[22], (DEPTH, N_HEADS_S)),
        "g_ssd": gain(ks[23], (DEPTH, D_INNER)),
        "w_pa": nrm(ks[24], (DEPTH, ATTN_WIDTH, D_MODEL), ATTN_WIDTH ** -0.5),
        "w_pb": nrm(ks[25], (DEPTH, D_INNER, D_MODEL), D_INNER ** -0.5),
        "w_out": nrm(ks[26], (DEPTH, D_MODEL, D_MODEL), D_MODEL ** -0.5),
        "w_gu": nrm(ks[27], (DEPTH, D_MODEL, 2 * D_FF), D_MODEL ** -0.5),
        "w_down": nrm(ks[28], (DEPTH, D_FF, D_MODEL), D_FF ** -0.5),
        "g_final": gain(ks[29], (D_MODEL,)),
    }


def reference(x_prompt, x_sample, cache_k, cache_v, state_conv, state_ssm, c_prompt, c_sample,
              w_ada, b_ada, g_mix, g_ffn, w_in, lambda_q1, lambda_k1, lambda_q2, lambda_k2, g_subln,
              conv_w, conv_b, dt_bias, a_log, d_skip, g_ssd, w_pa, w_pb, w_out, w_gu, w_down, g_final):
    bp, tp = x_prompt.shape[:2]
    ts = x_sample.shape[1]
    past = cache_k.shape[2]
    pos_p = jnp.arange(tp)
    pos_s = past + jnp.arange(ts)
    xp, xs = x_prompt, x_sample
    kp_l, vp_l, cp_l, sp_l, ks_l, vs_l, cs_l, ss_l = [], [], [], [], [], [], [], []
    for layer in range(DEPTH):
        lp = (w_ada[layer], b_ada[layer], g_mix[layer], g_ffn[layer], w_in[layer],
              lambda_q1[layer], lambda_k1[layer], lambda_q2[layer], lambda_k2[layer], g_subln[layer],
              conv_w[layer], conv_b[layer], dt_bias[layer], a_log[layer], d_skip[layer], g_ssd[layer],
              w_pa[layer], w_pb[layer], w_out[layer], w_gu[layer], w_down[layer])
        lam_init = 0.8 - 0.6 * math.exp(-0.3 * layer)
        zero_conv = jnp.zeros((bp, CONV_W - 1, CONV_DIM), x_prompt.dtype)
        zero_ssm = jnp.zeros((bp, N_HEADS_S, HEAD_DIM_S, D_STATE), x_prompt.dtype)
        xp, (k_p, v_p, c_p, s_p) = trunk_layer(xp, c_prompt, pos_p, None, None, zero_conv, zero_ssm, lp, lam_init)
        xs, (k_s, v_s, c_s, s_s) = trunk_layer(xs, c_sample, pos_s, cache_k[layer], cache_v[layer],
                                               state_conv[layer], state_ssm[layer], lp, lam_init)
        kp_l.append(k_p); vp_l.append(v_p); cp_l.append(c_p); sp_l.append(s_p)
        ks_l.append(k_s); vs_l.append(v_s); cs_l.append(c_s); ss_l.append(s_s)
    y_prompt = rmsnorm(xp, g_final)
    y_sample = rmsnorm(xs, g_final)
    new_k_prompt = jnp.stack(kp_l)
    new_v_prompt = jnp.stack(vp_l)
    new_conv_prompt = jnp.stack(cp_l)
    new_ssm_prompt = jnp.stack(sp_l)
    new_k_sample = jnp.stack(ks_l)
    new_v_sample = jnp.stack(vs_l)
    new_conv_sample = jnp.stack(cs_l)
    new_ssm_sample = jnp.stack(ss_l)
    return (y_prompt, y_sample, new_k_prompt, new_v_prompt, new_conv_prompt, new_ssm_prompt,
            new_k_sample, new_v_sample, new_conv_sample, new_ssm_sample)
```

```python
import functools
import math

import jax
import jax.numpy as jnp
from jax import lax
from jax.experimental import pallas as pl
from jax.experimental.pallas import tpu as pltpu

F32 = jnp.float32
BF16 = jnp.bfloat16

D_MODEL = 1024
N_HEADS_A = 8
HEAD_DIM_A = 64
V_DIM_A = 128
D_INNER = 2048
HEAD_DIM_S = 64
N_HEADS_S = 32
N_GROUPS_S = 4
GROUP_W = D_INNER // N_GROUPS_S
D_STATE = 128
CONV_W = 4
CONV_DIM = D_INNER + 2 * N_GROUPS_S * D_STATE
D_FF = 2816
CHUNK_A = 64
ROPE_THETA = 10000.0
EPS = 1e-5
SCALE_A = HEAD_DIM_A ** -0.5
OFF_DT = 8192
OFF_GATE = OFF_DT + N_HEADS_S

LANES = 128
SUBLANES = 8
V7X_VMEM_BYTES = 64 * 1024 * 1024
VMEM_LIMIT = 56 * 1024 * 1024
NEG = -1e30

ROW_TILE = 1024
COL_TILE = 1024
ATTN_Q_BLOCK = 256
ATTN_PAST_BLOCK = 256
SSD_CHUNK = 128
MIX_ROW_TILE = 512
FFN_ROW_TILE = 512
FFN_COL_CHUNK = 256


def _cparams(n_axes):
    return pltpu.CompilerParams(dimension_semantics=("arbitrary",) * n_axes,
                                vmem_limit_bytes=VMEM_LIMIT)


def _const_spec(shape):
    nd = len(shape)
    return pl.BlockSpec(shape, lambda *_: (0,) * nd, pipeline_mode=pl.Buffered(1))


def _row_tiling(n_batch, t, max_rows):
    if t >= max_rows:
        assert t % max_rows == 0
        return max_rows, 1, t // max_rows
    nb = 1
    for cand in range(1, n_batch + 1):
        if n_batch % cand == 0 and cand * t <= max_rows:
            nb = cand
    return nb * t, nb, 1


def _adaln_kernel(c_ref, w_ref, b_ref, o_ref):
    c = c_ref[...]
    s = (c * jax.nn.sigmoid(c)).astype(BF16)
    o_ref[...] = jnp.dot(s, w_ref[...].astype(BF16), preferred_element_type=F32) + b_ref[...]


def _adaln(c, w_ada, b_ada):
    n, d = c.shape
    cols = w_ada.shape[1]
    return pl.pallas_call(
        _adaln_kernel,
        grid=(cols // COL_TILE,),
        in_specs=[pl.BlockSpec((n, d), lambda j: (0, 0)),
                  pl.BlockSpec((d, COL_TILE), lambda j: (0, j)),
                  pl.BlockSpec((1, COL_TILE), lambda j: (0, j))],
        out_specs=pl.BlockSpec((n, COL_TILE), lambda j: (0, j)),
        out_shape=jax.ShapeDtypeStruct((n, cols), F32),
        compiler_params=_cparams(1),
        name="adaln",
    )(c, w_ada, b_ada.reshape(1, cols))


def _rope_block(x, cos, sin_signed, first_half):
    swapped = jnp.where(first_half, pltpu.roll(x, LANES - 32, 1), pltpu.roll(x, 32, 1))
    return x * cos + swapped * sin_signed


def _inproj_kernel(x_ref, sc_ref, sh_ref, g_ref, cos_ref, sin_ref, w_ref, wdt_ref,
                   q_ref, k_ref, v_ref, z_ref, xbc_ref, gate_ref, dt_ref, tail_ref,
                   h_scr, *, nb, rpb):
    j = pl.program_id(1)
    tm = x_ref.shape[0]

    @pl.when(j == 0)
    def _():
        x = x_ref[...]
        ms = jnp.mean(x * x, axis=-1, keepdims=True)
        xn = (x * lax.rsqrt(ms + EPS) * g_ref[...]).reshape(nb, rpb, D_MODEL)
        h = (xn * (1.0 + sc_ref[...]) + sh_ref[...]).reshape(tm, D_MODEL).astype(BF16)
        h_scr[...] = h
        dt_ref[...] = jnp.dot(h, wdt_ref[...], preferred_element_type=F32)

    acc = jnp.dot(h_scr[...], w_ref[...], preferred_element_type=F32)

    def rope(scale):
        cos = cos_ref[...]
        sin = sin_ref[...]
        lane = lax.broadcasted_iota(jnp.int32, (tm, LANES), 1)
        first_half = (lane % HEAD_DIM_A) < (HEAD_DIM_A // 2)
        blocks = []
        for cb in range(COL_TILE // LANES):
            xb = acc[:, cb * LANES:(cb + 1) * LANES]
            blocks.append(_rope_block(xb, cos, sin, first_half) * scale)
        return blocks

    @pl.when(j == 0)
    def _():
        for cb, blk in enumerate(rope(SCALE_A)):
            q_ref[:, cb * LANES:(cb + 1) * LANES] = blk.astype(BF16)

    @pl.when(j == 1)
    def _():
        for cb, blk in enumerate(rope(1.0)):
            k_ref[:, cb * LANES:(cb + 1) * LANES] = blk

    @pl.when(j == 2)
    def _():
        v_ref[...] = acc

    @pl.when((j == 3) | (j == 4))
    def _():
        z_ref[...] = acc.astype(BF16)

    @pl.when((j >= 5) & (j <= 7))
    def _():
        xbc_ref[...] = acc.astype(BF16)
        tail_ref[...] = acc.reshape(nb, rpb, COL_TILE)[:, rpb - SUBLANES:, :]

    @pl.when(j >= 8)
    def _():
        gate_ref[...] = acc.astype(BF16)


def _inproj(x2d, scale, shift, g_mix, cos_t, sin_t, w_main, w_dt, n_batch, t):
    n = x2d.shape[0]
    tm, nb, tpb = _row_tiling(n_batch, t, ROW_TILE)
    rpb = tm // nb
    n_col = w_main.shape[1] // COL_TILE
    rope_blocks = cos_t.shape[0] // tm

    def clamp(j, lo, cnt):
        return jnp.clip(j - lo, 0, cnt - 1)

    row = lambda i, j: (i, 0)
    out_shapes = (
        jax.ShapeDtypeStruct((n, D_MODEL), BF16),
        jax.ShapeDtypeStruct((n, D_MODEL), F32),
        jax.ShapeDtypeStruct((n, D_MODEL), F32),
        jax.ShapeDtypeStruct((n, D_INNER), BF16),
        jax.ShapeDtypeStruct((n, CONV_DIM), BF16),
        jax.ShapeDtypeStruct((n, 2 * D_MODEL), BF16),
        jax.ShapeDtypeStruct((n, LANES), F32),
        jax.ShapeDtypeStruct((n // rpb, SUBLANES, CONV_DIM), F32),
    )
    out_specs = (
        pl.BlockSpec((tm, COL_TILE), row),
        pl.BlockSpec((tm, COL_TILE), row),
        pl.BlockSpec((tm, COL_TILE), row),
        pl.BlockSpec((tm, COL_TILE), lambda i, j: (i, clamp(j, 3, 2))),
        pl.BlockSpec((tm, COL_TILE), lambda i, j: (i, clamp(j, 5, 3))),
        pl.BlockSpec((tm, COL_TILE), lambda i, j: (i, clamp(j, 8, 2))),
        pl.BlockSpec((tm, LANES), row),
        pl.BlockSpec((nb, SUBLANES, COL_TILE), lambda i, j: (i, 0, clamp(j, 5, 3))),
    )
    in_specs = [
        pl.BlockSpec((tm, D_MODEL), row),
        pl.BlockSpec((nb, 1, D_MODEL), lambda i, j: (i // tpb, 0, 0)),
        pl.BlockSpec((nb, 1, D_MODEL), lambda i, j: (i // tpb, 0, 0)),
        pl.BlockSpec((1, D_MODEL), lambda i, j: (0, 0)),
        pl.BlockSpec((tm, LANES), lambda i, j: (i % rope_blocks, 0)),
        pl.BlockSpec((tm, LANES), lambda i, j: (i % rope_blocks, 0)),
        pl.BlockSpec((D_MODEL, COL_TILE), lambda i, j: (0, j)),
        pl.BlockSpec((D_MODEL, LANES), lambda i, j: (0, 0)),
    ]
    return pl.pallas_call(
        functools.partial(_inproj_kernel, nb=nb, rpb=rpb),
        grid=(n // tm, n_col),
        in_specs=in_specs,
        out_specs=out_specs,
        out_shape=out_shapes,
        scratch_shapes=[pltpu.VMEM((tm, D_MODEL), BF16)],
        compiler_params=_cparams(2),
        name="inproj",
    )(x2d, scale, shift, g_mix, cos_t, sin_t, w_main, w_dt)


def _attn_kernel(*refs, t, past, tq, tkp, lam_init):
    if past:
        (lam_ref, gsub_ref, q_ref, k_ref, v_ref, pk_ref, pv_ref, o_ref,
         kb_scr, vb_scr, pkb_scr, pvb_scr, m1_scr, l1_scr, m2_scr, l2_scr, a1_scr, a2_scr) = refs
    else:
        (lam_ref, gsub_ref, q_ref, k_ref, v_ref, o_ref,
         kb_scr, vb_scr, m1_scr, l1_scr, m2_scr, l2_scr, a1_scr, a2_scr) = refs

    kb_scr[...] = k_ref[...].astype(BF16)
    vb_scr[...] = v_ref[...].astype(BF16)
    if past:
        pkb_scr[...] = pk_ref[...].astype(BF16)
        pvb_scr[...] = pv_ref[...].astype(BF16)

    lp = lam_ref[...]
    lam = (jnp.exp(jnp.sum(lp[0:1] * lp[1:2], axis=-1, keepdims=True))
           - jnp.exp(jnp.sum(lp[2:3] * lp[3:4], axis=-1, keepdims=True)) + lam_init)
    gsub = gsub_ref[...] * (1.0 - lam_init)

    lane = lax.broadcasted_iota(jnp.int32, (tq, LANES), 1)
    lo_half = lane < HEAD_DIM_A
    r_chunk = lax.broadcasted_iota(jnp.int32, (tq, tq), 0) // CHUNK_A
    c_chunk = lax.broadcasted_iota(jnp.int32, (tq, tq), 1) // CHUNK_A
    diag_mask = c_chunk <= r_chunk
    nt_dims = (((1,), (1,)), ((), ()))

    def q_block(qi, carry):
        qs = pl.multiple_of(qi * tq, tq)
        q = q_ref[pl.ds(qs, tq), :]
        zero = jnp.zeros_like(q)
        q_lo = jnp.where(lo_half, q, zero)
        q_hi = jnp.where(lo_half, zero, q)
        for m_scr, l_scr, a_scr in ((m1_scr, l1_scr, a1_scr), (m2_scr, l2_scr, a2_scr)):
            m_scr[...] = jnp.full(m_scr.shape, NEG, F32)
            l_scr[...] = jnp.zeros(l_scr.shape, F32)
            a_scr[...] = jnp.zeros(a_scr.shape, F32)

        def update(qh, kblk, vblk, mask, m_scr, l_scr, a_scr):
            s = lax.dot_general(qh, kblk, nt_dims, preferred_element_type=F32)
            if mask is not None:
                s = jnp.where(mask, s, NEG)
            m_old = m_scr[...]
            m_new = jnp.maximum(m_old, jnp.max(s, axis=-1, keepdims=True))
            alpha = jnp.exp(m_old - m_new)
            p = jnp.exp(s - m_new)
            l_scr[...] = alpha * l_scr[...] + jnp.sum(p, axis=-1, keepdims=True)
            a_scr[...] = alpha * a_scr[...] + jnp.dot(p.astype(BF16), vblk, preferred_element_type=F32)
            m_scr[...] = m_new

        def step(kblk, vblk, mask):
            update(q_lo, kblk, vblk, mask, m1_scr, l1_scr, a1_scr)
            update(q_hi, kblk, vblk, mask, m2_scr, l2_scr, a2_scr)

        if past:
            def past_body(jb, c):
                ks = pl.multiple_of(jb * tkp, tkp)
                step(pkb_scr[pl.ds(ks, tkp), :], pvb_scr[pl.ds(ks, tkp), :], None)
                return c
            lax.fori_loop(0, past // tkp, past_body, 0)

        def own_body(jb, c):
            ks = pl.multiple_of(jb * tq, tq)
            step(kb_scr[pl.ds(ks, tq), :], vb_scr[pl.ds(ks, tq), :], None)
            return c
        lax.fori_loop(0, qi, own_body, 0)
        step(kb_scr[pl.ds(qs, tq), :], vb_scr[pl.ds(qs, tq), :], diag_mask)

        o = a1_scr[...] / l1_scr[...] - lam * (a2_scr[...] / l2_scr[...])
        ms = jnp.mean(o * o, axis=-1, keepdims=True)
        o_ref[pl.ds(qs, tq), :] = (o * lax.rsqrt(ms + EPS) * gsub).astype(BF16)
        return carry

    lax.fori_loop(0, t // tq, q_block, 0)


def _attention(q, k, v, lam_rows, g_subln, n_batch, t, lam_init, past_k=None, past_v=None):
    past = 0 if past_k is None else past_k.shape[0] // n_batch
    tq = min(ATTN_Q_BLOCK, t)
    tkp = min(ATTN_PAST_BLOCK, past) if past else 0
    head = lambda b, h: (b, h)
    const = lambda b, h: (0, 0)
    in_specs = [pl.BlockSpec((SUBLANES, LANES), const),
                pl.BlockSpec((1, LANES), const),
                pl.BlockSpec((t, LANES), head),
                pl.BlockSpec((t, LANES), head),
                pl.BlockSpec((t, LANES), head)]
    args = [lam_rows, g_subln, q, k, v]
    scratch = [pltpu.VMEM((t, LANES), BF16), pltpu.VMEM((t, LANES), BF16)]
    if past:
        in_specs += [pl.BlockSpec((past, LANES), head), pl.BlockSpec((past, LANES), head)]
        args += [past_k, past_v]
        scratch += [pltpu.VMEM((past, LANES), BF16), pltpu.VMEM((past, LANES), BF16)]
    scratch += [pltpu.VMEM((tq, 1), F32), pltpu.VMEM((tq, 1), F32),
                pltpu.VMEM((tq, 1), F32), pltpu.VMEM((tq, 1), F32),
                pltpu.VMEM((tq, LANES), F32), pltpu.VMEM((tq, LANES), F32)]
    return pl.pallas_call(
        functools.partial(_attn_kernel, t=t, past=past, tq=tq, tkp=tkp, lam_init=lam_init),
        grid=(n_batch, N_HEADS_A),
        in_specs=in_specs,
        out_specs=pl.BlockSpec((t, LANES), head),
        out_shape=jax.ShapeDtypeStruct((n_batch * t, N_HEADS_A * V_DIM_A), BF16),
        scratch_shapes=scratch,
        compiler_params=_cparams(2),
        name="diff_attention",
    )(*args)


def _split_dot(a_bf16_exact, x, parts):
    out = None
    rem = x
    for _ in range(parts):
        piece = rem.astype(BF16)
        term = jnp.dot(a_bf16_exact, piece, preferred_element_type=F32)
        out = term if out is None else out + term
        rem = rem - piece.astype(F32)
    return out


def _ssd_kernel(*refs, chunk, has_init):
    if has_init:
        (z_ref, xbc_ref, dt_ref, cw_ref, cb_ref, dtb_ref, alog_ref, dskip_ref, gssd_ref, exp_ref,
         conv0_ref, ssm0_ref, ys_ref, sout_ref, ubuf, st_scr, y_scr) = refs
    else:
        (z_ref, xbc_ref, dt_ref, cw_ref, cb_ref, dtb_ref, alog_ref, dskip_ref, gssd_ref, exp_ref,
         ys_ref, sout_ref, ubuf, st_scr, y_scr) = refs
    c = pl.program_id(1)
    n_chunks = pl.num_programs(1)
    L = chunk
    n_blk = D_INNER // LANES

    @pl.when(c == 0)
    def _():
        if has_init:
            ubuf[0:SUBLANES, :] = conv0_ref[0]
            for jb in range(n_blk):
                st_scr[:, jb * LANES:(jb + 1) * LANES] = ssm0_ref[0, jb * LANES:(jb + 1) * LANES, :].T
        else:
            ubuf[0:SUBLANES, :] = jnp.zeros((SUBLANES, CONV_DIM), F32)
            st_scr[...] = jnp.zeros(st_scr.shape, F32)

    u = xbc_ref[...].astype(F32)
    ubuf[SUBLANES:SUBLANES + L, :] = u
    cw = cw_ref[...]
    conv = cb_ref[...] + u * cw[3:4]
    for jt in range(CONV_W - 1):
        conv = conv + ubuf[SUBLANES - 3 + jt:SUBLANES - 3 + jt + L, :] * cw[jt:jt + 1]
    ubuf[0:SUBLANES, :] = u[L - SUBLANES:, :]
    xc = conv * jax.nn.sigmoid(conv)
    xs = xc[:, :D_INNER]
    b_all = xc[:, D_INNER:D_INNER + N_GROUPS_S * D_STATE]
    c_all = xc[:, D_INNER + N_GROUPS_S * D_STATE:]

    dt = jax.nn.softplus(dt_ref[...] + dtb_ref[...])
    d_a = dt * (-jnp.exp(alog_ref[...]))
    row = lax.broadcasted_iota(jnp.int32, (L, L), 0)
    col = lax.broadcasted_iota(jnp.int32, (L, L), 1)
    causal = col <= row
    tril = jnp.where(causal, 1.0, 0.0).astype(BF16)
    cum = _split_dot(tril, d_a, 3)
    cum_last = cum[L - 1:L, :]
    to_end = jnp.exp(cum_last - cum)
    e_cum = jnp.exp(cum)
    cum_t = cum.T
    dt_t = dt.T
    expand = exp_ref[...]
    ecum_x = jnp.dot(e_cum.astype(BF16), expand, preferred_element_type=F32)
    w_x = jnp.dot((dt * to_end).astype(BF16), expand, preferred_element_type=F32)
    dec = jnp.broadcast_to(jnp.exp(cum_last), (SUBLANES, LANES))
    dec_x = _split_dot_rhs(dec, expand)[0:1, :]

    lane = lax.broadcasted_iota(jnp.int32, (L, LANES), 1)
    lo_half = lane < HEAD_DIM_S

    for g in range(N_GROUPS_S):
        gs = slice(g * GROUP_W, (g + 1) * GROUP_W)
        c_g = c_all[:, g * D_STATE:(g + 1) * D_STATE].astype(BF16)
        b_gt = b_all[:, g * D_STATE:(g + 1) * D_STATE].T.astype(BF16)
        cb = jnp.dot(c_g, b_gt, preferred_element_type=F32)
        st_g = st_scr[:, gs]
        y_off = jnp.dot(c_g, st_g.astype(BF16), preferred_element_type=F32) * ecum_x[:, gs]
        for pp in range(GROUP_W // LANES):
            blk = slice(g * GROUP_W + pp * LANES, g * GROUP_W + (pp + 1) * LANES)
            x_pair = xs[:, blk].astype(BF16)
            ys_pair = []
            for hh in range(2):
                h = (g * GROUP_W + pp * LANES) // HEAD_DIM_S + hh
                seg = cum[:, h:h + 1] - cum_t[h:h + 1, :]
                decay = jnp.exp(jnp.where(causal, seg, NEG))
                m = (cb * decay * dt_t[h:h + 1, :]).astype(BF16)
                ys_pair.append(jnp.dot(m, x_pair, preferred_element_type=F32))
            y_scr[:, pp * LANES:(pp + 1) * LANES] = jnp.where(lo_half, ys_pair[0], ys_pair[1])
        xs_g = xs[:, gs]
        y = y_scr[...] + y_off + dskip_ref[:, gs] * xs_g
        zg = z_ref[:, gs].astype(F32)
        gated = y * (zg * jax.nn.sigmoid(zg))
        ms = jnp.mean(gated * gated, axis=-1, keepdims=True)
        ys_ref[:, gs] = (gated * lax.rsqrt(ms + EPS) * gssd_ref[:, gs]).astype(BF16)
        xw = (xs_g * w_x[:, gs]).astype(BF16)
        st_scr[:, gs] = st_g * dec_x[:, gs] + jnp.dot(b_gt, xw, preferred_element_type=F32)

    @pl.when(c == n_chunks - 1)
    def _():
        for jb in range(n_blk):
            sout_ref[0, jb * LANES:(jb + 1) * LANES, :] = st_scr[:, jb * LANES:(jb + 1) * LANES].T


def _split_dot_rhs(x, b_bf16_exact):
    hi = x.astype(BF16)
    lo = (x - hi.astype(F32)).astype(BF16)
    return (jnp.dot(hi, b_bf16_exact, preferred_element_type=F32)
            + jnp.dot(lo, b_bf16_exact, preferred_element_type=F32))


def _ssd(z, xbc, dt, conv_w, conv_b, dt_bias, a_log, dskip_x, g_ssd, expand, n_batch, t,
         conv0=None, ssm0=None):
    has_init = conv0 is not None
    chunk = min(SSD_CHUNK, t)
    nc = t // chunk
    rows = lambda b, c: (b * nc + c, 0)
    const = lambda b, c: (0, 0)
    in_specs = [pl.BlockSpec((chunk, D_INNER), rows),
                pl.BlockSpec((chunk, CONV_DIM), rows),
                pl.BlockSpec((chunk, LANES), rows),
                pl.BlockSpec((CONV_W, CONV_DIM), const),
                pl.BlockSpec((1, CONV_DIM), const),
                pl.BlockSpec((1, LANES), const),
                pl.BlockSpec((1, LANES), const),
                pl.BlockSpec((1, D_INNER), const),
                pl.BlockSpec((1, D_INNER), const),
                pl.BlockSpec((LANES, D_INNER), const)]
    args = [z, xbc, dt, conv_w, conv_b, dt_bias, a_log, dskip_x, g_ssd, expand]
    if has_init:
        in_specs += [pl.BlockSpec((1, SUBLANES, CONV_DIM), lambda b, c: (b, 0, 0)),
                     pl.BlockSpec((1, D_INNER, D_STATE), lambda b, c: (b, 0, 0))]
        args += [conv0, ssm0]
    return pl.pallas_call(
        functools.partial(_ssd_kernel, chunk=chunk, has_init=has_init),
        grid=(n_batch, nc),
        in_specs=in_specs,
        out_specs=(pl.BlockSpec((chunk, D_INNER), rows),
                   pl.BlockSpec((1, D_INNER, D_STATE), lambda b, c: (b, 0, 0))),
        out_shape=(jax.ShapeDtypeStruct((n_batch * t, D_INNER), BF16),
                   jax.ShapeDtypeStruct((n_batch, D_INNER, D_STATE), F32)),
        scratch_shapes=[pltpu.VMEM((chunk + SUBLANES, CONV_DIM), F32),
                        pltpu.VMEM((D_STATE, D_INNER), F32),
                        pltpu.VMEM((chunk, GROUP_W), F32)],
        compiler_params=_cparams(2),
        name="ssd_scan",
    )(*args)


def _mixout_kernel(x_ref, o_ref, ys_ref, gate_ref, gt_ref, wpa_ref, wpb_ref, wout_ref, x1_ref, *, nb, rpb):
    tm = x_ref.shape[0]
    y_a = jnp.dot(o_ref[...], wpa_ref[...], preferred_element_type=F32)
    y_b = jnp.dot(ys_ref[...], wpb_ref[...], preferred_element_type=F32)
    g_a = gate_ref[:, :D_MODEL].astype(F32)
    g_b = gate_ref[:, D_MODEL:].astype(F32)
    merged = (jax.nn.sigmoid(g_a) * y_a + jax.nn.sigmoid(g_b) * y_b).astype(BF16)
    out = jnp.dot(merged, wout_ref[...], preferred_element_type=F32)
    x1 = x_ref[...].reshape(nb, rpb, D_MODEL) + gt_ref[...] * out.reshape(nb, rpb, D_MODEL)
    x1_ref[...] = x1.reshape(tm, D_MODEL)


def _mixout(x2d, o, ys, gates, gt_m, w_pa, w_pb, w_out, n_batch, t):
    n = x2d.shape[0]
    tm, nb, tpb = _row_tiling(n_batch, t, MIX_ROW_TILE)
    row = lambda i: (i, 0)
    return pl.pallas_call(
        functools.partial(_mixout_kernel, nb=nb, rpb=tm // nb),
        grid=(n // tm,),
        in_specs=[pl.BlockSpec((tm, D_MODEL), row),
                  pl.BlockSpec((tm, D_MODEL), row),
                  pl.BlockSpec((tm, D_INNER), row),
                  pl.BlockSpec((tm, 2 * D_MODEL), row),
                  pl.BlockSpec((nb, 1, D_MODEL), lambda i: (i // tpb, 0, 0)),
                  _const_spec(w_pa.shape), _const_spec(w_pb.shape), _const_spec(w_out.shape)],
        out_specs=pl.BlockSpec((tm, D_MODEL), row),
        out_shape=jax.ShapeDtypeStruct((n, D_MODEL), F32),
        compiler_params=_cparams(1),
        name="mixout",
    )(x2d, o, ys, gates, gt_m, w_pa, w_pb, w_out)


def _ffn_kernel(x1_ref, sc_ref, sh_ref, gt_ref, gffn_ref, gfin_ref, wg_ref, wu_ref, wd_ref, y_ref, *, nb, rpb):
    tm = x1_ref.shape[0]
    x1 = x1_ref[...]
    ms = jnp.mean(x1 * x1, axis=-1, keepdims=True)
    xn = (x1 * lax.rsqrt(ms + EPS) * gffn_ref[...]).reshape(nb, rpb, D_MODEL)
    h2 = (xn * (1.0 + sc_ref[...]) + sh_ref[...]).reshape(tm, D_MODEL).astype(BF16)
    acc = jnp.zeros((tm, D_MODEL), F32)
    for cc in range(D_FF // FFN_COL_CHUNK):
        cs = slice(cc * FFN_COL_CHUNK, (cc + 1) * FFN_COL_CHUNK)
        gate = jnp.dot(h2, wg_ref[:, cs], preferred_element_type=F32)
        up = jnp.dot(h2, wu_ref[:, cs], preferred_element_type=F32)
        act = (gate * jax.nn.sigmoid(gate) * up).astype(BF16)
        acc = acc + jnp.dot(act, wd_ref[cs, :], preferred_element_type=F32)
    x2 = x1.reshape(nb, rpb, D_MODEL) + gt_ref[...] * acc.reshape(nb, rpb, D_MODEL)
    x2 = x2.reshape(tm, D_MODEL)
    ms2 = jnp.mean(x2 * x2, axis=-1, keepdims=True)
    y_ref[...] = x2 * lax.rsqrt(ms2 + EPS) * gfin_ref[...]


def _ffn(x1, sc_f, sh_f, gt_f, g_ffn, g_final, w_gate, w_up, w_down, n_batch, t):
    n = x1.shape[0]
    tm, nb, tpb = _row_tiling(n_batch, t, FFN_ROW_TILE)
    row = lambda i: (i, 0)
    mod = pl.BlockSpec((nb, 1, D_MODEL), lambda i: (i // tpb, 0, 0))
    vec = pl.BlockSpec((1, D_MODEL), lambda i: (0, 0))
    return pl.pallas_call(
        functools.partial(_ffn_kernel, nb=nb, rpb=tm // nb),
        grid=(n // tm,),
        in_specs=[pl.BlockSpec((tm, D_MODEL), row), mod, mod, mod, vec, vec,
                  _const_spec(w_gate.shape), _const_spec(w_up.shape), _const_spec(w_down.shape)],
        out_specs=pl.BlockSpec((tm, D_MODEL), row),
        out_shape=jax.ShapeDtypeStruct((n, D_MODEL), F32),
        compiler_params=_cparams(1),
        name="ffn_final",
    )(x1, sc_f, sh_f, gt_f, g_ffn, g_final, w_gate, w_up, w_down)


def _rope_tables(pos, rows):
    half = HEAD_DIM_A // 2
    inv = ROPE_THETA ** (-jnp.arange(half, dtype=F32) / half)
    ang = pos.astype(F32)[:, None] * inv[None, :]
    cos = jnp.cos(ang)
    sin = jnp.sin(ang)
    cos_t = jnp.tile(cos, (1, LANES // half))
    sin_t = jnp.tile(jnp.concatenate([-sin, sin], axis=-1), (1, LANES // HEAD_DIM_A))
    reps = max(1, rows // pos.shape[0])
    return jnp.tile(cos_t, (reps, 1)), jnp.tile(sin_t, (reps, 1))


def _layer(x, mod, pos, weights, lam_init, g_final, past_k=None, past_v=None, conv0=None, ssm0=None):
    n_batch, t, _ = x.shape
    n = n_batch * t
    x2d = x.reshape(n, D_MODEL)
    sh_m, sc_m, gt_m, sh_f, sc_f, gt_f = [m.reshape(n_batch, 1, D_MODEL) for m in jnp.split(mod, 6, axis=-1)]
    tm, _, _ = _row_tiling(n_batch, t, ROW_TILE)
    cos_t, sin_t = _rope_tables(pos, tm)

    q, k, v, z, xbc, gates, dt, tail = _inproj(
        x2d, sc_m, sh_m, weights["g_mix"], cos_t, sin_t, weights["w_main"], weights["w_dt"], n_batch, t)

    if past_k is not None:
        past = past_k.shape[1]
        pk = past_k.reshape(n_batch * past, D_MODEL)
        pv = past_v.reshape(n_batch * past, D_MODEL)
        o = _attention(q, k, v, weights["lam_rows"], weights["g_subln"], n_batch, t, lam_init, pk, pv)
    else:
        o = _attention(q, k, v, weights["lam_rows"], weights["g_subln"], n_batch, t, lam_init)

    if conv0 is not None:
        conv0 = jnp.pad(conv0, ((0, 0), (SUBLANES - (CONV_W - 1), 0), (0, 0)))
        ssm0 = ssm0.reshape(n_batch, D_INNER, D_STATE)
    ys, s_new = _ssd(z, xbc, dt, weights["conv_w"], weights["conv_b"], weights["dt_bias"], weights["a_log"],
                     weights["dskip_x"], weights["g_ssd"], weights["expand"], n_batch, t, conv0, ssm0)

    x1 = _mixout(x2d, o, ys, gates, gt_m, weights["w_pa"], weights["w_pb"], weights["w_out"], n_batch, t)
    y = _ffn(x1, sc_f, sh_f, gt_f, weights["g_ffn"], g_final, weights["w_gate"], weights["w_up"],
             weights["w_down"], n_batch, t)

    return (y.reshape(n_batch, t, D_MODEL),
            k.reshape(n_batch, t, 2 * N_HEADS_A, HEAD_DIM_A),
            v.reshape(n_batch, t, N_HEADS_A, V_DIM_A),
            tail.reshape(n_batch, -1, SUBLANES, CONV_DIM)[:, -1, SUBLANES - (CONV_W - 1):, :],
            s_new.reshape(n_batch, N_HEADS_S, HEAD_DIM_S, D_STATE))


def _prep_weights(layer, w_ada, b_ada, g_mix, g_ffn, w_in, lq1, lk1, lq2, lk2, g_subln, conv_w, conv_b,
                  dt_bias, a_log, d_skip, g_ssd, w_pa, w_pb, w_out, w_gu, w_down):
    w_in_l = w_in[layer]
    lanes_pad = LANES - N_HEADS_S
    head_of_channel = jnp.arange(D_INNER) // HEAD_DIM_S
    lam_rows = jnp.stack([lq1[layer], lk1[layer], lq2[layer], lk2[layer]])
    return {
        "g_mix": g_mix[layer].reshape(1, D_MODEL),
        "g_ffn": g_ffn[layer].reshape(1, D_MODEL),
        "w_main": jnp.concatenate([w_in_l[:, :OFF_DT], w_in_l[:, OFF_GATE:]], axis=1).astype(BF16),
        "w_dt": jnp.pad(w_in_l[:, OFF_DT:OFF_GATE], ((0, 0), (0, lanes_pad))).astype(BF16),
        "lam_rows": jnp.pad(lam_rows, ((0, SUBLANES - 4), (0, LANES - HEAD_DIM_A))),
        "g_subln": g_subln[layer].reshape(1, V_DIM_A),
        "conv_w": conv_w[layer],
        "conv_b": conv_b[layer].reshape(1, CONV_DIM),
        "dt_bias": jnp.pad(dt_bias[layer], (0, lanes_pad)).reshape(1, LANES),
        "a_log": jnp.pad(a_log[layer], (0, lanes_pad)).reshape(1, LANES),
        "dskip_x": jnp.repeat(d_skip[layer], HEAD_DIM_S).reshape(1, D_INNER),
        "g_ssd": g_ssd[layer].reshape(1, D_INNER),
        "expand": (jnp.arange(LANES)[:, None] == head_of_channel[None, :]).astype(BF16),
        "w_pa": w_pa[layer].astype(BF16),
        "w_pb": w_pb[layer].astype(BF16),
        "w_out": w_out[layer].astype(BF16),
        "w_gate": w_gu[layer][:, :D_FF].astype(BF16),
        "w_up": w_gu[layer][:, D_FF:].astype(BF16),
        "w_down": w_down[layer].astype(BF16),
    }


def kernel(x_prompt, x_sample, cache_k, cache_v, state_conv, state_ssm, c_prompt, c_sample, w_ada, b_ada, g_mix, g_ffn, w_in, lambda_q1, lambda_k1, lambda_q2, lambda_k2, g_subln, conv_w, conv_b, dt_bias, a_log, d_skip, g_ssd, w_pa, w_pb, w_out, w_gu, w_down, g_final):
    depth = w_in.shape[0]
    bp, tp = x_prompt.shape[:2]
    bs, ts = x_sample.shape[:2]
    past = cache_k.shape[2]
    pos_p = jnp.arange(tp)
    pos_s = past + jnp.arange(ts)
    g_fin = g_final.reshape(1, D_MODEL)
    xp, xs = x_prompt, x_sample
    outs_p, outs_s = [], []
    for layer in range(depth):
        wts = _prep_weights(layer, w_ada, b_ada, g_mix, g_ffn, w_in, lambda_q1, lambda_k1, lambda_q2,
                            lambda_k2, g_subln, conv_w, conv_b, dt_bias, a_log, d_skip, g_ssd,
                            w_pa, w_pb, w_out, w_gu, w_down)
        lam_init = 0.8 - 0.6 * math.exp(-0.3 * layer)
        mod = _adaln(jnp.concatenate([c_prompt, c_sample], axis=0), w_ada[layer], b_ada[layer])
        assert depth == 1
        res_p = _layer(xp, mod[:bp], pos_p, wts, lam_init, g_fin)
        res_s = _layer(xs, mod[bp:], pos_s, wts, lam_init, g_fin,
                       past_k=cache_k[layer], past_v=cache_v[layer],
                       conv0=state_conv[layer], ssm0=state_ssm[layer])
        xp, xs = res_p[0], res_s[0]
        outs_p.append(res_p[1:])
        outs_s.append(res_s[1:])
    stack = lambda outs, idx: jnp.stack([o[idx] for o in outs])
    return (xp, xs,
            stack(outs_p, 0), stack(outs_p, 1), stack(outs_p, 2), stack(outs_p, 3),
            stack(outs_s, 0), stack(outs_s, 1), stack(outs_s, 2), stack(outs_s, 3))
```

```python
import functools
import math

import jax
import jax.numpy as jnp
from jax import lax
from jax.experimental import pallas as pl
from jax.experimental.pallas import tpu as pltpu

F32 = jnp.float32
BF16 = jnp.bfloat16

D_MODEL = 1024
N_HEADS_A = 8
HEAD_DIM_A = 64
V_DIM_A = 128
D_INNER = 2048
HEAD_DIM_S = 64
N_HEADS_S = 32
N_GROUPS_S = 4
GROUP_W = D_INNER // N_GROUPS_S
D_STATE = 128
CONV_W = 4
CONV_DIM = D_INNER + 2 * N_GROUPS_S * D_STATE
D_FF = 2816
CHUNK_A = 64
ROPE_THETA = 10000.0
EPS = 1e-5
SCALE_A = HEAD_DIM_A ** -0.5
Q_SCALE = SCALE_A * math.log2(math.e)
OFF_DT = 8192
OFF_GATE = OFF_DT + N_HEADS_S

LANES = 128
SUBLANES = 8
V7X_VMEM_BYTES = 64 * 1024 * 1024
VMEM_LIMIT = 56 * 1024 * 1024
NEG = -1e30

ROW_TILE = 1024
COL_TILE = 1024
ATTN_Q_BLOCK = 256
SSD_CHUNK = 128
MIX_ROW_TILE = 512
FFN_ROW_TILE = 512
FFN_COL_CHUNK = 256


def _cparams(n_axes):
    return pltpu.CompilerParams(dimension_semantics=("arbitrary",) * n_axes,
                                vmem_limit_bytes=VMEM_LIMIT)


def _const_spec(shape):
    nd = len(shape)
    return pl.BlockSpec(shape, lambda *_: (0,) * nd, pipeline_mode=pl.Buffered(1))


def _row_tiling(n_batch, t, max_rows):
    if t >= max_rows:
        assert t % max_rows == 0
        return max_rows, 1, t // max_rows
    nb = 1
    for cand in range(1, n_batch + 1):
        if n_batch % cand == 0 and cand * t <= max_rows:
            nb = cand
    return nb * t, nb, 1


def _adaln_kernel(c_ref, w_ref, b_ref, o_ref):
    c = c_ref[...]
    s = (c * jax.nn.sigmoid(c)).astype(BF16)
    o_ref[...] = jnp.dot(s, w_ref[...].astype(BF16), preferred_element_type=F32) + b_ref[...]


def _adaln(c, w_ada, b_ada):
    n, d = c.shape
    cols = w_ada.shape[1]
    return pl.pallas_call(
        _adaln_kernel,
        grid=(cols // COL_TILE,),
        in_specs=[pl.BlockSpec((n, d), lambda j: (0, 0)),
                  pl.BlockSpec((d, COL_TILE), lambda j: (0, j)),
                  pl.BlockSpec((1, COL_TILE), lambda j: (0, j))],
        out_specs=pl.BlockSpec((n, COL_TILE), lambda j: (0, j)),
        out_shape=jax.ShapeDtypeStruct((n, cols), F32),
        compiler_params=_cparams(1),
        name="adaln",
    )(c, w_ada, b_ada.reshape(1, cols))


def _rope_block(x, cos, sin_signed, first_half):
    swapped = jnp.where(first_half, pltpu.roll(x, LANES - 32, 1), pltpu.roll(x, 32, 1))
    return x * cos + swapped * sin_signed


def _modulated_norm(x_ref, sc_ref, sh_ref, g_ref, nb, rpb):
    x = x_ref[...]
    ms = jnp.mean(x * x, axis=-1, keepdims=True)
    xn = (x * lax.rsqrt(ms + EPS) * g_ref[...]).reshape(nb, rpb, D_MODEL)
    return (xn * (1.0 + sc_ref[...]) + sh_ref[...]).reshape(nb * rpb, D_MODEL).astype(BF16)


def _qkv_kernel(x_ref, sc_ref, sh_ref, g_ref, cos_ref, sin_ref, w_ref,
                q_ref, kf_ref, kb_ref, vf_ref, vb_ref, h_scr, *, nb, rpb, head_major):
    j = pl.program_id(1)
    tm = x_ref.shape[0]

    def proj():
        return jnp.dot(h_scr[...], w_ref[...], preferred_element_type=F32)

    def rope_blocks(acc, scale):
        cos = cos_ref[...]
        sin = sin_ref[...]
        lane = lax.broadcasted_iota(jnp.int32, (tm, LANES), 1)
        first_half = (lane % HEAD_DIM_A) < (HEAD_DIM_A // 2)
        for cb in range(COL_TILE // LANES):
            xb = acc[:, cb * LANES:(cb + 1) * LANES]
            yield cb, _rope_block(xb, cos, sin, first_half) * scale

    @pl.when(j == 0)
    def _():
        h_scr[...] = _modulated_norm(x_ref, sc_ref, sh_ref, g_ref, nb, rpb)
        for cb, blk in rope_blocks(proj(), Q_SCALE):
            q_ref[:, cb * LANES:(cb + 1) * LANES] = blk.astype(BF16)

    @pl.when(j == 1)
    def _():
        for cb, blk in rope_blocks(proj(), 1.0):
            cols = slice(cb * LANES, (cb + 1) * LANES)
            if head_major:
                blk_t = blk.T
                kf_ref[0, cols, :] = blk_t
                kb_ref[0, cols, :] = blk_t.astype(BF16)
            else:
                kf_ref[:, cols] = blk
                kb_ref[:, cols] = blk.astype(BF16)

    @pl.when(j == 2)
    def _():
        acc = proj()
        vb_ref[...] = acc.astype(BF16)
        if head_major:
            for hd in range(N_HEADS_A):
                vf_ref[:, hd, :] = acc[:, hd * V_DIM_A:(hd + 1) * V_DIM_A]
        else:
            vf_ref[...] = acc


def _zxg_kernel(x_ref, sc_ref, sh_ref, g_ref, w_ref, wdt_ref,
                z_ref, xbc_ref, gate_ref, dt_ref, tail_ref, h_scr, *, nb, rpb):
    j = pl.program_id(1)

    def proj():
        return jnp.dot(h_scr[...], w_ref[...], preferred_element_type=F32)

    @pl.when(j == 0)
    def _():
        h = _modulated_norm(x_ref, sc_ref, sh_ref, g_ref, nb, rpb)
        h_scr[...] = h
        dt_ref[...] = jnp.dot(h, wdt_ref[...], preferred_element_type=F32)

    @pl.when(j < 2)
    def _():
        z_ref[...] = proj().astype(BF16)

    @pl.when((j >= 2) & (j < 5))
    def _():
        acc = proj()
        xbc_ref[...] = acc.astype(BF16)
        tail_ref[...] = acc.reshape(nb, rpb, COL_TILE)[:, rpb - SUBLANES:, :]

    @pl.when(j >= 5)
    def _():
        gate_ref[...] = proj().astype(BF16)


def _inproj(x2d, scale, shift, g_mix, cos_t, sin_t, w_main, w_dt, n_batch, t):
    n = x2d.shape[0]
    tm, nb, tpb = _row_tiling(n_batch, t, ROW_TILE)
    rpb = tm // nb
    head_major = nb == 1
    rope_blocks = cos_t.shape[0] // tm
    n_qkv = 3
    n_zxg = w_main.shape[1] // COL_TILE - n_qkv

    def clamp(j, lo, cnt):
        return jnp.clip(j - lo, 0, cnt - 1)

    row = lambda i, j: (i, 0)
    mod_spec = pl.BlockSpec((nb, 1, D_MODEL), lambda i, j: (i // tpb, 0, 0))
    common_in = [pl.BlockSpec((tm, D_MODEL), row), mod_spec, mod_spec,
                 pl.BlockSpec((1, D_MODEL), lambda i, j: (0, 0))]
    if head_major:
        k_shape = (n_batch, D_MODEL, t)
        k_spec = pl.BlockSpec((1, D_MODEL, tm), lambda i, j: (i // tpb, 0, i % tpb))
        v_shape = (n, N_HEADS_A, V_DIM_A)
        v_spec = pl.BlockSpec((tm, N_HEADS_A, V_DIM_A), lambda i, j: (i, 0, 0))
    else:
        k_shape = (n, D_MODEL)
        k_spec = pl.BlockSpec((tm, COL_TILE), row)
        v_shape = (n, D_MODEL)
        v_spec = pl.BlockSpec((tm, COL_TILE), row)
    q, k_f32, k_bf16, v_f32, v_bf16 = pl.pallas_call(
        functools.partial(_qkv_kernel, nb=nb, rpb=rpb, head_major=head_major),
        grid=(n // tm, n_qkv),
        in_specs=common_in + [
            pl.BlockSpec((tm, LANES), lambda i, j: (i % rope_blocks, 0)),
            pl.BlockSpec((tm, LANES), lambda i, j: (i % rope_blocks, 0)),
            pl.BlockSpec((D_MODEL, COL_TILE), lambda i, j: (0, j))],
        out_specs=(pl.BlockSpec((tm, COL_TILE), row), k_spec, k_spec, v_spec,
                   pl.BlockSpec((tm, COL_TILE), row)),
        out_shape=(
            jax.ShapeDtypeStruct((n, D_MODEL), BF16),
            jax.ShapeDtypeStruct(k_shape, F32),
            jax.ShapeDtypeStruct(k_shape, BF16),
            jax.ShapeDtypeStruct(v_shape, F32),
            jax.ShapeDtypeStruct((n, D_MODEL), BF16)),
        scratch_shapes=[pltpu.VMEM((tm, D_MODEL), BF16)],
        compiler_params=_cparams(2),
        name="inproj_qkv",
    )(x2d, scale, shift, g_mix, cos_t, sin_t, w_main)

    z, xbc, gates, dt, tail = pl.pallas_call(
        functools.partial(_zxg_kernel, nb=nb, rpb=rpb),
        grid=(n // tm, n_zxg),
        in_specs=common_in + [
            pl.BlockSpec((D_MODEL, COL_TILE), lambda i, j: (0, j + n_qkv)),
            pl.BlockSpec((D_MODEL, LANES), lambda i, j: (0, 0))],
        out_specs=(
            pl.BlockSpec((tm, COL_TILE), lambda i, j: (i, clamp(j, 0, 2))),
            pl.BlockSpec((tm, COL_TILE), lambda i, j: (i, clamp(j, 2, 3))),
            pl.BlockSpec((tm, COL_TILE), lambda i, j: (i, clamp(j, 5, 2))),
            pl.BlockSpec((tm, LANES), row),
            pl.BlockSpec((nb, SUBLANES, COL_TILE), lambda i, j: (i, 0, clamp(j, 2, 3)))),
        out_shape=(
            jax.ShapeDtypeStruct((n, D_INNER), BF16),
            jax.ShapeDtypeStruct((n, CONV_DIM), BF16),
            jax.ShapeDtypeStruct((n, 2 * D_MODEL), BF16),
            jax.ShapeDtypeStruct((n, LANES), F32),
            jax.ShapeDtypeStruct((n // rpb, SUBLANES, CONV_DIM), F32)),
        scratch_shapes=[pltpu.VMEM((tm, D_MODEL), BF16)],
        compiler_params=_cparams(2),
        name="inproj_zxg",
    )(x2d, scale, shift, g_mix, w_main, w_dt)
    return q, k_f32, k_bf16, v_f32, v_bf16, z, xbc, gates, dt, tail


def _attn_kernel(*refs, t, past, tq, lam_init, k_transposed):
    lam_ref, gsub_ref, q_ref, k_ref, v_ref = refs[:5]
    if past:
        pk_ref, pv_ref, o_ref, pkb_scr, pvb_scr = refs[5:]
        pkb_scr[...] = pk_ref[0].astype(BF16)
        pvb_scr[...] = pv_ref[...].astype(BF16)
    else:
        o_ref = refs[5]

    lp = lam_ref[...]
    lam = (jnp.exp(jnp.sum(lp[0:1] * lp[1:2], axis=-1, keepdims=True))
           - jnp.exp(jnp.sum(lp[2:3] * lp[3:4], axis=-1, keepdims=True)) + lam_init)
    gsub = gsub_ref[...] * (1.0 - lam_init)

    lane = lax.broadcasted_iota(jnp.int32, (tq, LANES), 1)
    lo_half = lane < HEAD_DIM_A
    r_chunk = lax.broadcasted_iota(jnp.int32, (tq, tq), 0) // CHUNK_A
    c_chunk = lax.broadcasted_iota(jnp.int32, (tq, tq), 1) // CHUNK_A
    diag_mask = c_chunk <= r_chunk
    nt_dims = (((1,), (1,)), ((), ()))

    def own_scores(qh, lo, hi):
        if k_transposed:
            return jnp.dot(qh, k_ref[0, :, lo:hi], preferred_element_type=F32)
        return lax.dot_general(qh, k_ref[lo:hi, :], nt_dims, preferred_element_type=F32)

    for qi in range(t // tq):
        n0 = qi * tq
        q = q_ref[n0:n0 + tq, :]
        zero = jnp.zeros_like(q)
        heads = []
        for qh in (jnp.where(lo_half, q, zero), jnp.where(lo_half, zero, q)):
            s_parts, v_parts = [], []
            if past:
                s_parts.append(jnp.dot(qh, pkb_scr[...], preferred_element_type=F32))
                v_parts.append(pvb_scr[...])
            if n0:
                s_parts.append(own_scores(qh, 0, n0))
                v_parts.append(v_ref[0:n0, :])
            s_parts.append(jnp.where(diag_mask, own_scores(qh, n0, n0 + tq), NEG))
            v_parts.append(v_ref[n0:n0 + tq, :])
            m = functools.reduce(jnp.maximum, [jnp.max(s, axis=-1, keepdims=True) for s in s_parts])
            p_parts = [jnp.exp2(s - m) for s in s_parts]
            l = functools.reduce(jnp.add, [jnp.sum(p, axis=-1, keepdims=True) for p in p_parts])
            acc = functools.reduce(jnp.add, [jnp.dot(p.astype(BF16), vv, preferred_element_type=F32)
                                             for p, vv in zip(p_parts, v_parts)])
            heads.append(acc / l)
        o = heads[0] - lam * heads[1]
        ms = jnp.mean(o * o, axis=-1, keepdims=True)
        o_ref[n0:n0 + tq, :] = (o * lax.rsqrt(ms + EPS) * gsub).astype(BF16)


def _attention(q, k, v, lam_rows, g_subln, n_batch, t, lam_init, past_kt=None, past_v=None):
    past = 0 if past_kt is None else past_kt.shape[2]
    k_transposed = k.ndim == 3
    tq = min(ATTN_Q_BLOCK, t)
    head = lambda b, h: (b, h)
    head_t = lambda b, h: (b, h, 0)
    const = lambda b, h: (0, 0)
    in_specs = [pl.BlockSpec((SUBLANES, LANES), const),
                pl.BlockSpec((1, LANES), const),
                pl.BlockSpec((t, LANES), head),
                pl.BlockSpec((1, LANES, t), head_t) if k_transposed else pl.BlockSpec((t, LANES), head),
                pl.BlockSpec((t, LANES), head)]
    args = [lam_rows, g_subln, q, k, v]
    scratch = []
    if past:
        in_specs += [pl.BlockSpec((1, LANES, past), head_t), pl.BlockSpec((past, LANES), head)]
        args += [past_kt, past_v]
        scratch += [pltpu.VMEM((LANES, past), BF16), pltpu.VMEM((past, LANES), BF16)]
    return pl.pallas_call(
        functools.partial(_attn_kernel, t=t, past=past, tq=tq, lam_init=lam_init, k_transposed=k_transposed),
        grid=(n_batch, N_HEADS_A),
        in_specs=in_specs,
        out_specs=pl.BlockSpec((t, LANES), head),
        out_shape=jax.ShapeDtypeStruct((n_batch * t, N_HEADS_A * V_DIM_A), BF16),
        scratch_shapes=scratch,
        compiler_params=_cparams(2),
        name="diff_attention",
    )(*args)


def _split_dot(a_bf16_exact, x, parts):
    out = None
    rem = x
    for _ in range(parts):
        piece = rem.astype(BF16)
        term = jnp.dot(a_bf16_exact, piece, preferred_element_type=F32)
        out = term if out is None else out + term
        rem = rem - piece.astype(F32)
    return out


def _split_dot_rhs(x, b_bf16_exact):
    hi = x.astype(BF16)
    lo = (x - hi.astype(F32)).astype(BF16)
    return (jnp.dot(hi, b_bf16_exact, preferred_element_type=F32)
            + jnp.dot(lo, b_bf16_exact, preferred_element_type=F32))


def _ssd_kernel(*refs, chunk, has_init):
    if has_init:
        (z_ref, xbc_ref, dt_ref, cw_ref, cb_ref, dtb_ref, alog_ref, dskip_ref, gssd_ref, exp_ref,
         conv0_ref, ssm0_ref, ys_ref, sout_ref, ubuf, st_scr, y_scr) = refs
    else:
        (z_ref, xbc_ref, dt_ref, cw_ref, cb_ref, dtb_ref, alog_ref, dskip_ref, gssd_ref, exp_ref,
         ys_ref, sout_ref, ubuf, st_scr, y_scr) = refs
    c = pl.program_id(1)
    n_chunks = pl.num_programs(1)
    L = chunk
    n_blk = D_INNER // LANES

    @pl.when(c == 0)
    def _():
        if has_init:
            ubuf[0:SUBLANES, :] = conv0_ref[0]
            for jb in range(n_blk):
                st_scr[:, jb * LANES:(jb + 1) * LANES] = ssm0_ref[0, jb * LANES:(jb + 1) * LANES, :].T
        else:
            ubuf[0:SUBLANES, :] = jnp.zeros((SUBLANES, CONV_DIM), F32)
            st_scr[...] = jnp.zeros(st_scr.shape, F32)

    u = xbc_ref[...].astype(F32)
    ubuf[SUBLANES:SUBLANES + L, :] = u
    cw = cw_ref[...]
    conv = cb_ref[...] + u * cw[3:4]
    for jt in range(CONV_W - 1):
        conv = conv + ubuf[SUBLANES - 3 + jt:SUBLANES - 3 + jt + L, :] * cw[jt:jt + 1]
    ubuf[0:SUBLANES, :] = u[L - SUBLANES:, :]
    xc = conv * jax.nn.sigmoid(conv)
    xs = xc[:, :D_INNER]
    b_all = xc[:, D_INNER:D_INNER + N_GROUPS_S * D_STATE]
    c_all = xc[:, D_INNER + N_GROUPS_S * D_STATE:]

    dt = jax.nn.softplus(dt_ref[...] + dtb_ref[...])
    d_a = dt * (-jnp.exp(alog_ref[...]))
    row = lax.broadcasted_iota(jnp.int32, (L, L), 0)
    col = lax.broadcasted_iota(jnp.int32, (L, L), 1)
    causal = col <= row
    tril = jnp.where(causal, 1.0, 0.0).astype(BF16)
    cum = _split_dot(tril, d_a, 3)
    cum_last = cum[L - 1:L, :]
    to_end = jnp.exp(cum_last - cum)
    e_cum = jnp.exp(cum)
    cum_t = cum.T
    dt_t = dt.T
    expand = exp_ref[...]
    ecum_x = jnp.dot(e_cum.astype(BF16), expand, preferred_element_type=F32)
    w_x = jnp.dot((dt * to_end).astype(BF16), expand, preferred_element_type=F32)
    dec = jnp.broadcast_to(jnp.exp(cum_last), (SUBLANES, LANES))
    dec_x = _split_dot_rhs(dec, expand)[0:1, :]

    lane = lax.broadcasted_iota(jnp.int32, (L, LANES), 1)
    lo_half = lane < HEAD_DIM_S

    for g in range(N_GROUPS_S):
        gs = slice(g * GROUP_W, (g + 1) * GROUP_W)
        c_g = c_all[:, g * D_STATE:(g + 1) * D_STATE].astype(BF16)
        b_gt = b_all[:, g * D_STATE:(g + 1) * D_STATE].T.astype(BF16)
        cb = jnp.dot(c_g, b_gt, preferred_element_type=F32)
        st_g = st_scr[:, gs]
        y_off = jnp.dot(c_g, st_g.astype(BF16), preferred_element_type=F32) * ecum_x[:, gs]
        for pp in range(GROUP_W // LANES):
            blk = slice(g * GROUP_W + pp * LANES, g * GROUP_W + (pp + 1) * LANES)
            x_pair = xs[:, blk].astype(BF16)
            ys_pair = []
            for hh in range(2):
                h = (g * GROUP_W + pp * LANES) // HEAD_DIM_S + hh
                seg = cum[:, h:h + 1] - cum_t[h:h + 1, :]
                decay = jnp.exp(jnp.where(causal, seg, NEG))
                m = (cb * decay * dt_t[h:h + 1, :]).astype(BF16)
                ys_pair.append(jnp.dot(m, x_pair, preferred_element_type=F32))
            y_scr[:, pp * LANES:(pp + 1) * LANES] = jnp.where(lo_half, ys_pair[0], ys_pair[1])
        xs_g = xs[:, gs]
        y = y_scr[...] + y_off + dskip_ref[:, gs] * xs_g
        zg = z_ref[:, gs].astype(F32)
        gated = y * (zg * jax.nn.sigmoid(zg))
        ms = jnp.mean(gated * gated, axis=-1, keepdims=True)
        ys_ref[:, gs] = (gated * lax.rsqrt(ms + EPS) * gssd_ref[:, gs]).astype(BF16)
        xw = (xs_g * w_x[:, gs]).astype(BF16)
        st_scr[:, gs] = st_g * dec_x[:, gs] + jnp.dot(b_gt, xw, preferred_element_type=F32)

    @pl.when(c == n_chunks - 1)
    def _():
        for jb in range(n_blk):
            sout_ref[0, jb * LANES:(jb + 1) * LANES, :] = st_scr[:, jb * LANES:(jb + 1) * LANES].T


def _ssd(z, xbc, dt, conv_w, conv_b, dt_bias, a_log, dskip_x, g_ssd, expand, n_batch, t,
         conv0=None, ssm0=None):
    has_init = conv0 is not None
    chunk = min(SSD_CHUNK, t)
    nc = t // chunk
    rows = lambda b, c: (b * nc + c, 0)
    const = lambda b, c: (0, 0)
    in_specs = [pl.BlockSpec((chunk, D_INNER), rows),
                pl.BlockSpec((chunk, CONV_DIM), rows),
                pl.BlockSpec((chunk, LANES), rows),
                pl.BlockSpec((CONV_W, CONV_DIM), const),
                pl.BlockSpec((1, CONV_DIM), const),
                pl.BlockSpec((1, LANES), const),
                pl.BlockSpec((1, LANES), const),
                pl.BlockSpec((1, D_INNER), const),
                pl.BlockSpec((1, D_INNER), const),
                pl.BlockSpec((LANES, D_INNER), const)]
    args = [z, xbc, dt, conv_w, conv_b, dt_bias, a_log, dskip_x, g_ssd, expand]
    if has_init:
        in_specs += [pl.BlockSpec((1, SUBLANES, CONV_DIM), lambda b, c: (b, 0, 0)),
                     pl.BlockSpec((1, D_INNER, D_STATE), lambda b, c: (b, 0, 0))]
        args += [conv0, ssm0]
    return pl.pallas_call(
        functools.partial(_ssd_kernel, chunk=chunk, has_init=has_init),
        grid=(n_batch, nc),
        in_specs=in_specs,
        out_specs=(pl.BlockSpec((chunk, D_INNER), rows),
                   pl.BlockSpec((1, D_INNER, D_STATE), lambda b, c: (b, 0, 0))),
        out_shape=(jax.ShapeDtypeStruct((n_batch * t, D_INNER), BF16),
                   jax.ShapeDtypeStruct((n_batch, D_INNER, D_STATE), F32)),
        scratch_shapes=[pltpu.VMEM((chunk + SUBLANES, CONV_DIM), F32),
                        pltpu.VMEM((D_STATE, D_INNER), F32),
                        pltpu.VMEM((chunk, GROUP_W), F32)],
        compiler_params=_cparams(2),
        name="ssd_scan",
    )(*args)


def _mixout_kernel(x_ref, o_ref, ys_ref, gate_ref, gt_ref, wpa_ref, wpb_ref, wout_ref, x1_ref, *, nb, rpb):
    tm = x_ref.shape[0]
    y_a = jnp.dot(o_ref[...], wpa_ref[...], preferred_element_type=F32)
    y_b = jnp.dot(ys_ref[...], wpb_ref[...], preferred_element_type=F32)
    g_a = gate_ref[:, :D_MODEL].astype(F32)
    g_b = gate_ref[:, D_MODEL:].astype(F32)
    merged = (jax.nn.sigmoid(g_a) * y_a + jax.nn.sigmoid(g_b) * y_b).astype(BF16)
    out = jnp.dot(merged, wout_ref[...], preferred_element_type=F32)
    x1 = x_ref[...].reshape(nb, rpb, D_MODEL) + gt_ref[...] * out.reshape(nb, rpb, D_MODEL)
    x1_ref[...] = x1.reshape(tm, D_MODEL)


def _mixout(x2d, o, ys, gates, gt_m, w_pa, w_pb, w_out, n_batch, t):
    n = x2d.shape[0]
    tm, nb, tpb = _row_tiling(n_batch, t, MIX_ROW_TILE)
    row = lambda i: (i, 0)
    return pl.pallas_call(
        functools.partial(_mixout_kernel, nb=nb, rpb=tm // nb),
        grid=(n // tm,),
        in_specs=[pl.BlockSpec((tm, D_MODEL), row),
                  pl.BlockSpec((tm, D_MODEL), row),
                  pl.BlockSpec((tm, D_INNER), row),
                  pl.BlockSpec((tm, 2 * D_MODEL), row),
                  pl.BlockSpec((nb, 1, D_MODEL), lambda i: (i // tpb, 0, 0)),
                  _const_spec(w_pa.shape), _const_spec(w_pb.shape), _const_spec(w_out.shape)],
        out_specs=pl.BlockSpec((tm, D_MODEL), row),
        out_shape=jax.ShapeDtypeStruct((n, D_MODEL), F32),
        compiler_params=_cparams(1),
        name="mixout",
    )(x2d, o, ys, gates, gt_m, w_pa, w_pb, w_out)


def _ffn_kernel(x1_ref, sc_ref, sh_ref, gt_ref, gffn_ref, gfin_ref, wg_ref, wu_ref, wd_ref, y_ref, *, nb, rpb):
    tm = x1_ref.shape[0]
    x1 = x1_ref[...]
    ms = jnp.mean(x1 * x1, axis=-1, keepdims=True)
    xn = (x1 * lax.rsqrt(ms + EPS) * gffn_ref[...]).reshape(nb, rpb, D_MODEL)
    h2 = (xn * (1.0 + sc_ref[...]) + sh_ref[...]).reshape(tm, D_MODEL).astype(BF16)
    acc = jnp.zeros((tm, D_MODEL), F32)
    for cc in range(D_FF // FFN_COL_CHUNK):
        cs = slice(cc * FFN_COL_CHUNK, (cc + 1) * FFN_COL_CHUNK)
        gate = jnp.dot(h2, wg_ref[:, cs], preferred_element_type=F32)
        up = jnp.dot(h2, wu_ref[:, cs], preferred_element_type=F32)
        act = (gate * jax.nn.sigmoid(gate) * up).astype(BF16)
        acc = acc + jnp.dot(act, wd_ref[cs, :], preferred_element_type=F32)
    x2 = x1.reshape(nb, rpb, D_MODEL) + gt_ref[...] * acc.reshape(nb, rpb, D_MODEL)
    x2 = x2.reshape(tm, D_MODEL)
    ms2 = jnp.mean(x2 * x2, axis=-1, keepdims=True)
    y_ref[...] = x2 * lax.rsqrt(ms2 + EPS) * gfin_ref[...]


def _ffn(x1, sc_f, sh_f, gt_f, g_ffn, g_final, w_gate, w_up, w_down, n_batch, t):
    n = x1.shape[0]
    tm, nb, tpb = _row_tiling(n_batch, t, FFN_ROW_TILE)
    row = lambda i: (i, 0)
    mod = pl.BlockSpec((nb, 1, D_MODEL), lambda i: (i // tpb, 0, 0))
    vec = pl.BlockSpec((1, D_MODEL), lambda i: (0, 0))
    return pl.pallas_call(
        functools.partial(_ffn_kernel, nb=nb, rpb=tm // nb),
        grid=(n // tm,),
        in_specs=[pl.BlockSpec((tm, D_MODEL), row), mod, mod, mod, vec, vec,
                  _const_spec(w_gate.shape), _const_spec(w_up.shape), _const_spec(w_down.shape)],
        out_specs=pl.BlockSpec((tm, D_MODEL), row),
        out_shape=jax.ShapeDtypeStruct((n, D_MODEL), F32),
        compiler_params=_cparams(1),
        name="ffn_final",
    )(x1, sc_f, sh_f, gt_f, g_ffn, g_final, w_gate, w_up, w_down)


def _rope_tables(pos, rows):
    half = HEAD_DIM_A // 2
    inv = ROPE_THETA ** (-jnp.arange(half, dtype=F32) / half)
    ang = pos.astype(F32)[:, None] * inv[None, :]
    cos = jnp.cos(ang)
    sin = jnp.sin(ang)
    cos_t = jnp.tile(cos, (1, LANES // half))
    sin_t = jnp.tile(jnp.concatenate([-sin, sin], axis=-1), (1, LANES // HEAD_DIM_A))
    reps = max(1, rows // pos.shape[0])
    return jnp.tile(cos_t, (reps, 1)), jnp.tile(sin_t, (reps, 1))


def _layer(x, mod, pos, weights, lam_init, g_final, past_k=None, past_v=None, conv0=None, ssm0=None):
    n_batch, t, _ = x.shape
    n = n_batch * t
    x2d = x.reshape(n, D_MODEL)
    sh_m, sc_m, gt_m, sh_f, sc_f, gt_f = [m.reshape(n_batch, 1, D_MODEL) for m in jnp.split(mod, 6, axis=-1)]
    tm, _, _ = _row_tiling(n_batch, t, ROW_TILE)
    cos_t, sin_t = _rope_tables(pos, tm)

    q, k_f32, k_bf16, v_f32, v_bf16, z, xbc, gates, dt, tail = _inproj(
        x2d, sc_m, sh_m, weights["g_mix"], cos_t, sin_t, weights["w_main"], weights["w_dt"], n_batch, t)

    if past_k is not None:
        past = past_k.shape[1]
        pk_t = jnp.transpose(past_k, (0, 2, 3, 1)).reshape(n_batch, D_MODEL, past)
        pv = past_v.reshape(n_batch * past, D_MODEL)
        o = _attention(q, k_bf16, v_bf16, weights["lam_rows"], weights["g_subln"], n_batch, t, lam_init, pk_t, pv)
    else:
        o = _attention(q, k_bf16, v_bf16, weights["lam_rows"], weights["g_subln"], n_batch, t, lam_init)

    if conv0 is not None:
        conv0 = jnp.pad(conv0, ((0, 0), (SUBLANES - (CONV_W - 1), 0), (0, 0)))
        ssm0 = ssm0.reshape(n_batch, D_INNER, D_STATE)
    ys, s_new = _ssd(z, xbc, dt, weights["conv_w"], weights["conv_b"], weights["dt_bias"], weights["a_log"],
                     weights["dskip_x"], weights["g_ssd"], weights["expand"], n_batch, t, conv0, ssm0)

    x1 = _mixout(x2d, o, ys, gates, gt_m, weights["w_pa"], weights["w_pb"], weights["w_out"], n_batch, t)
    y = _ffn(x1, sc_f, sh_f, gt_f, weights["g_ffn"], g_final, weights["w_gate"], weights["w_up"],
             weights["w_down"], n_batch, t)

    if k_f32.ndim == 3:
        k_out = jnp.transpose(k_f32.reshape(n_batch, 2 * N_HEADS_A, HEAD_DIM_A, t), (0, 3, 1, 2))
    else:
        k_out = k_f32.reshape(n_batch, t, 2 * N_HEADS_A, HEAD_DIM_A)
    return (y.reshape(n_batch, t, D_MODEL),
            k_out,
            v_f32.reshape(n_batch, t, N_HEADS_A, V_DIM_A),
            tail.reshape(n_batch, -1, SUBLANES, CONV_DIM)[:, -1, SUBLANES - (CONV_W - 1):, :],
            s_new.reshape(n_batch, N_HEADS_S, HEAD_DIM_S, D_STATE))


def _prep_weights(layer, w_ada, b_ada, g_mix, g_ffn, w_in, lq1, lk1, lq2, lk2, g_subln, conv_w, conv_b,
                  dt_bias, a_log, d_skip, g_ssd, w_pa, w_pb, w_out, w_gu, w_down):
    w_in_l = w_in[layer]
    lanes_pad = LANES - N_HEADS_S
    head_of_channel = jnp.arange(D_INNER) // HEAD_DIM_S
    lam_rows = jnp.stack([lq1[layer], lk1[layer], lq2[layer], lk2[layer]])
    return {
        "g_mix": g_mix[layer].reshape(1, D_MODEL),
        "g_ffn": g_ffn[layer].reshape(1, D_MODEL),
        "w_main": jnp.concatenate([w_in_l[:, :OFF_DT], w_in_l[:, OFF_GATE:]], axis=1).astype(BF16),
        "w_dt": jnp.pad(w_in_l[:, OFF_DT:OFF_GATE], ((0, 0), (0, lanes_pad))).astype(BF16),
        "lam_rows": jnp.pad(lam_rows, ((0, SUBLANES - 4), (0, LANES - HEAD_DIM_A))),
        "g_subln": g_subln[layer].reshape(1, V_DIM_A),
        "conv_w": conv_w[layer],
        "conv_b": conv_b[layer].reshape(1, CONV_DIM),
        "dt_bias": jnp.pad(dt_bias[layer], (0, lanes_pad)).reshape(1, LANES),
        "a_log": jnp.pad(a_log[layer], (0, lanes_pad)).reshape(1, LANES),
        "dskip_x": jnp.repeat(d_skip[layer], HEAD_DIM_S).reshape(1, D_INNER),
        "g_ssd": g_ssd[layer].reshape(1, D_INNER),
        "expand": (jnp.arange(LANES)[:, None] == head_of_channel[None, :]).astype(BF16),
        "w_pa": w_pa[layer].astype(BF16),
        "w_pb": w_pb[layer].astype(BF16),
        "w_out": w_out[layer].astype(BF16),
        "w_gate": w_gu[layer][:, :D_FF].astype(BF16),
        "w_up": w_gu[layer][:, D_FF:].astype(BF16),
        "w_down": w_down[layer].astype(BF16),
    }


def kernel(x_prompt, x_sample, cache_k, cache_v, state_conv, state_ssm, c_prompt, c_sample, w_ada, b_ada, g_mix, g_ffn, w_in, lambda_q1, lambda_k1, lambda_q2, lambda_k2, g_subln, conv_w, conv_b, dt_bias, a_log, d_skip, g_ssd, w_pa, w_pb, w_out, w_gu, w_down, g_final):
    depth = w_in.shape[0]
    bp, tp = x_prompt.shape[:2]
    bs, ts = x_sample.shape[:2]
    past = cache_k.shape[2]
    pos_p = jnp.arange(tp)
    pos_s = past + jnp.arange(ts)
    g_fin = g_final.reshape(1, D_MODEL)
    xp, xs = x_prompt, x_sample
    outs_p, outs_s = [], []
    for layer in range(depth):
        wts = _prep_weights(layer, w_ada, b_ada, g_mix, g_ffn, w_in, lambda_q1, lambda_k1, lambda_q2,
                            lambda_k2, g_subln, conv_w, conv_b, dt_bias, a_log, d_skip, g_ssd,
                            w_pa, w_pb, w_out, w_gu, w_down)
        lam_init = 0.8 - 0.6 * math.exp(-0.3 * layer)
        mod = _adaln(jnp.concatenate([c_prompt, c_sample], axis=0), w_ada[layer], b_ada[layer])
        assert depth == 1
        res_p = _layer(xp, mod[:bp], pos_p, wts, lam_init, g_fin)
        res_s = _layer(xs, mod[bp:], pos_s, wts, lam_init, g_fin,
                       past_k=cache_k[layer], past_v=cache_v[layer],
                       conv0=state_conv[layer], ssm0=state_ssm[layer])
        xp, xs = res_p[0], res_s[0]
        outs_p.append(res_p[1:])
        outs_s.append(res_s[1:])
    stack = lambda outs, idx: jnp.stack([o[idx] for o in outs])
    return (xp, xs,
            stack(outs_p, 0), stack(outs_p, 1), stack(outs_p, 2), stack(outs_p, 3),
            stack(outs_s, 0), stack(outs_s, 1), stack(outs_s, 2), stack(outs_s, 3))
```
